```python
import math
import jax, jax.numpy as jnp
from jax import lax
import numpy as np

D_MODEL = 1024
BATCH = 4
SEQ = 4096
DEPTH = 1

MEM_LEN = 256
EPS = 1e-6
ATTN_BLOCK = 128
MLA_HEADS = 8
MLA_NOPE = 64
MLA_ROPE = 32
MLA_V = 64
MLA_Q_RANK = 384
MLA_KV_RANK = 256
ROPE_THETA = 10000.0
DIL_HEADS = 8
DIL_HEAD_DIM = 64
DIL_CONFIGS = ((128, 1), (512, 4), (2048, 16))
MLA_WIDTH = MLA_HEADS * MLA_V
DIL_WIDTH = DIL_HEADS * DIL_HEAD_DIM
MIX_WIDTH = MLA_WIDTH + DIL_WIDTH
IN_COLS = MLA_Q_RANK + MLA_KV_RANK + MLA_ROPE + 3 * DIL_WIDTH
MEM_HEADS = 4
MEM_HEAD_DIM = D_MODEL // MEM_HEADS
PEER_HEADS = 8
PEER_KEYS = 128
PEER_N = PEER_KEYS * PEER_KEYS
PEER_QDIM = 256
PEER_TOPK = 16
PEER_CHUNK = 128

kernel_name = 'hybrid_mla_dilated_peer_block'


def rms_norm(x, g):
    xf = x.astype(jnp.float32)
    y = xf * lax.rsqrt(jnp.mean(xf * xf, axis=-1, keepdims=True) + EPS)
    return (y * g.astype(jnp.float32)).astype(x.dtype)


def alibi_slopes(n):
    return jnp.exp2(-8.0 * jnp.arange(1, n + 1, dtype=jnp.float32) / n)


def rope_tables(seq, dim):
    inv = 1.0 / (ROPE_THETA ** (jnp.arange(0, dim, 2, dtype=jnp.float32) / dim))
    ang = jnp.arange(seq, dtype=jnp.float32)[:, None] * inv[None, :]
    return jnp.cos(ang), jnp.sin(ang)


def apply_rope(x, cos, sin):
    x1, x2 = jnp.split(x.astype(jnp.float32), 2, axis=-1)
    return jnp.concatenate([x1 * cos - x2 * sin, x2 * cos + x1 * sin], axis=-1).astype(x.dtype)


def mla_attention(c_q, c_kv, k_rope, q_norm, kv_norm, w_uq, w_ukv):
    B, S, _ = c_q.shape
    H, dqk = MLA_HEADS, MLA_NOPE + MLA_ROPE
    q = (rms_norm(c_q, q_norm) @ w_uq).reshape(B, S, H, dqk)
    kv = (rms_norm(c_kv, kv_norm) @ w_ukv).reshape(B, S, H, MLA_NOPE + MLA_V)
    k_nope, v = kv[..., :MLA_NOPE], kv[..., MLA_NOPE:]
    cos, sin = rope_tables(S, MLA_ROPE)
    q_rope = apply_rope(q[..., MLA_NOPE:], cos[:, None], sin[:, None])
    k_rope = apply_rope(k_rope, cos, sin)
    q = jnp.concatenate([q[..., :MLA_NOPE], q_rope], axis=-1)
    k = jnp.concatenate([k_nope, jnp.broadcast_to(k_rope[:, :, None, :], (B, S, H, MLA_ROPE))], axis=-1)
    scale = dqk ** -0.5
    nb = S // ATTN_BLOCK
    q_blocks = q.reshape(B, nb, ATTN_BLOCK, H, dqk).transpose(1, 0, 2, 3, 4)
    key_pos = jnp.arange(S)

    def attend(args):
        qb, n = args
        s = jnp.einsum('bqhd,bkhd->bhqk', qb, k).astype(jnp.float32) * scale
        q_pos = n * ATTN_BLOCK + jnp.arange(ATTN_BLOCK)
        s = jnp.where(key_pos[None, :] <= q_pos[:, None], s, -jnp.inf)
        p = jax.nn.softmax(s, axis=-1).astype(v.dtype)
        return jnp.einsum('bhqk,bkhd->bqhd', p, v)

    o = lax.map(attend, (q_blocks, jnp.arange(nb)))
    return o.transpose(1, 0, 2, 3, 4).reshape(B, S, H * MLA_V)


def dilated_branch(q, k, v, window, dilation, slopes):
    B, S, H, Dh = q.shape
    W = window // dilation
    L = S // dilation
    nb = -(-L // W)
    Lp = nb * W

    def to_sub(t):
        t = t.reshape(B, L, dilation, H, Dh).transpose(0, 2, 3, 1, 4)
        return jnp.pad(t, ((0, 0), (0, 0), (0, 0), (0, Lp - L), (0, 0)))

    def band(t):
        tp = jnp.pad(t, ((0, 0), (0, 0), (0, 0), (W, 0), (0, 0)))
        prev = tp[..., :Lp, :].reshape(B, dilation, H, nb, W, Dh)
        cur = t.reshape(B, dilation, H, nb, W, Dh)
        return jnp.concatenate([prev, cur], axis=4)

    qs, ks, vs = to_sub(q), to_sub(k), to_sub(v)
    qb = qs.reshape(B, dilation, H, nb, W, Dh)
    kb, vb = band(ks), band(vs)
    s = jnp.einsum('brhnqd,brhnkd->brhnqk', qb, kb).astype(jnp.float32) * (Dh ** -0.5)
    qi = jnp.arange(W)[:, None]
    kj = jnp.arange(2 * W)[None, :]
    steps = W + qi - kj
    blk = jnp.arange(nb)[:, None, None]
    valid = (steps >= 0) & (steps <= W) & ((blk > 0) | (kj >= W))
    bias = -slopes[:, None, None, None] * (steps * dilation).astype(jnp.float32)
    s = jnp.where(valid, s + bias, -jnp.inf)
    lse = jax.nn.logsumexp(s, axis=-1)
    p = jnp.exp(s - lse[..., None]).astype(v.dtype)
    o = jnp.einsum('brhnqk,brhnkd->brhnqd', p, vb)
    o = o.reshape(B, dilation, H, Lp, Dh)[:, :, :, :L].transpose(0, 3, 1, 2, 4).reshape(B, S, H, Dh)
    lse = lse.reshape(B, dilation, H, Lp)[:, :, :, :L].transpose(0, 3, 1, 2).reshape(B, S, H)
    return o, lse


def dilated_attention(q, k, v):
    B, S, H, Dh = q.shape
    slopes = alibi_slopes(H)
    results = [dilated_branch(q, k, v, w, d, slopes) for (w, d) in DIL_CONFIGS]
    outs = jnp.stack([r[0] for r in results], axis=0)
    lses = jnp.stack([r[1] for r in results], axis=0)
    wts = jax.nn.softmax(lses, axis=0).astype(outs.dtype)
    o = jnp.einsum('gbsh,gbshd->bshd', wts, outs)
    return o.reshape(B, S, H * Dh)


def memory_attention(hn, memn, w_q, w_kv, w_o):
    B, S, _ = hn.shape
    M = memn.shape[1]
    q = (hn @ w_q).reshape(B, S, MEM_HEADS, MEM_HEAD_DIM)
    kv = (memn @ w_kv).reshape(B, M, 2, MEM_HEADS, MEM_HEAD_DIM)
    k, v = kv[:, :, 0], kv[:, :, 1]
    s = jnp.einsum('bshd,bmhd->bhsm', q, k).astype(jnp.float32) * (MEM_HEAD_DIM ** -0.5)
    p = jax.nn.softmax(s, axis=-1).astype(v.dtype)
    o = jnp.einsum('bhsm,bmhd->bshd', p, v).reshape(B, S, MEM_HEADS * MEM_HEAD_DIM)
    return o @ w_o


def peer_ffn(xn, w_q, sub_keys, u_tab, v_tab):
    B, S, D = xn.shape
    C, Hh, K = PEER_CHUNK, PEER_HEADS, PEER_TOPK
    half = PEER_QDIM // 2
    xt = xn.reshape((B * S) // C, C, D)

    def chunk(xc):
        q = (xc @ w_q).reshape(C, Hh, 2, half).astype(jnp.float32)
        s = jnp.einsum('chpd,hpnd->chpn', q, sub_keys.astype(jnp.float32))
        top_s, top_i = lax.top_k(s, K)
        cand = top_s[:, :, 0, :, None] + top_s[:, :, 1, None, :]
        best_s, best_j = lax.top_k(cand.reshape(C, Hh, K * K), K)
        ia = jnp.take_along_axis(top_i[:, :, 0], best_j // K, axis=-1)
        ib = jnp.take_along_axis(top_i[:, :, 1], best_j % K, axis=-1)
        expert = ia * PEER_KEYS + ib
        g = jax.nn.softmax(best_s, axis=-1)
        u = u_tab[expert]
        a = jax.nn.gelu(jnp.einsum('cd,chkd->chk', xc, u).astype(jnp.float32), approximate=False)
        coeff = (g * a).astype(xc.dtype)
        return jnp.einsum('chk,chkd->cd', coeff, v_tab[expert])

    return lax.map(chunk, xt).reshape(B, S, D)


def setup_inputs(seed: int = 0) -> dict:
    key = jax.random.key(seed)
    ks = jax.random.split(key, 22)
    f32 = jnp.float32

    def nrm(k, shape, scale):
        return jax.random.normal(k, shape, f32) * scale

    def gain(k, shape):
        return 1.0 + 0.02 * jax.random.normal(k, shape, f32)

    L = DEPTH
    return {
        'x': nrm(ks[0], (BATCH, SEQ, D_MODEL), 1.0),
        'mem': nrm(ks[1], (BATCH, MEM_LEN, D_MODEL), 1.0),
        'norm_mix': gain(ks[2], (L, D_MODEL)),
        'w_in': nrm(ks[3], (L, D_MODEL, IN_COLS), D_MODEL ** -0.5),
        'mla_q_norm': gain(ks[4], (L, MLA_Q_RANK)),
        'mla_kv_norm': gain(ks[5], (L, MLA_KV_RANK)),
        'mla_w_uq': nrm(ks[6], (L, MLA_Q_RANK, MLA_HEADS * (MLA_NOPE + MLA_ROPE)), MLA_Q_RANK ** -0.5),
        'mla_w_ukv': nrm(ks[7], (L, MLA_KV_RANK, MLA_HEADS * (MLA_NOPE + MLA_V)), MLA_KV_RANK ** -0.5),
        'grp_norm_mla': gain(ks[8], (L, MLA_WIDTH)),
        'grp_norm_dil': gain(ks[9], (L, DIL_WIDTH)),
        'w_out': nrm(ks[10], (L, MIX_WIDTH, D_MODEL), MIX_WIDTH ** -0.5),
        'norm_mem_q': gain(ks[11], (L, D_MODEL)),
        'norm_mem_kv': gain(ks[12], (L, D_MODEL)),
        'w_mem_q': nrm(ks[13], (L, D_MODEL, MEM_HEADS * MEM_HEAD_DIM), D_MODEL ** -0.5),
        'w_mem_kv': nrm(ks[14], (L, D_MODEL, 2 * MEM_HEADS * MEM_HEAD_DIM), D_MODEL ** -0.5),
        'w_mem_o': nrm(ks[15], (L, MEM_HEADS * MEM_HEAD_DIM, D_MODEL), D_MODEL ** -0.5),
        'norm_ffn': gain(ks[16], (L, D_MODEL)),
        'peer_w_q': nrm(ks[17], (L, D_MODEL, PEER_HEADS * PEER_QDIM), D_MODEL ** -0.5),
        'peer_sub_keys': nrm(ks[18], (L, PEER_HEADS, 2, PEER_KEYS, PEER_QDIM // 2), (PEER_QDIM // 2) ** -0.5),
        'peer_u': nrm(ks[19], (L, PEER_N, D_MODEL), D_MODEL ** -0.5),
        'peer_v': nrm(ks[20], (L, PEER_N, D_MODEL), PEER_HEADS ** -0.5),
        'norm_final': gain(ks[21], (D_MODEL,)),
    }


def reference(x, mem, norm_mix, w_in, mla_q_norm, mla_kv_norm, mla_w_uq, mla_w_ukv,
              grp_norm_mla, grp_norm_dil, w_out, norm_mem_q, norm_mem_kv, w_mem_q, w_mem_kv,
              w_mem_o, norm_ffn, peer_w_q, peer_sub_keys, peer_u, peer_v, norm_final):
    B, S, _ = x.shape
    o1 = MLA_Q_RANK
    o2 = o1 + MLA_KV_RANK
    o3 = o2 + MLA_ROPE
    o4 = o3 + DIL_WIDTH
    o5 = o4 + DIL_WIDTH
    h = x
    for l in range(DEPTH):
        proj = rms_norm(h, norm_mix[l]) @ w_in[l]
        c_q, c_kv, k_rope, dq, dk, dv = jnp.split(proj, [o1, o2, o3, o4, o5], axis=-1)
        o_mla = mla_attention(c_q, c_kv, k_rope, mla_q_norm[l], mla_kv_norm[l], mla_w_uq[l], mla_w_ukv[l])
        dshape = (B, S, DIL_HEADS, DIL_HEAD_DIM)
        o_dil = dilated_attention(dq.reshape(dshape), dk.reshape(dshape), dv.reshape(dshape))
        mixed = jnp.concatenate([rms_norm(o_mla, grp_norm_mla[l]), rms_norm(o_dil, grp_norm_dil[l])], axis=-1)
        h = h + mixed @ w_out[l]
        h = h + memory_attention(rms_norm(h, norm_mem_q[l]), rms_norm(mem, norm_mem_kv[l]),
                                 w_mem_q[l], w_mem_kv[l], w_mem_o[l])
        h = h + peer_ffn(rms_norm(h, norm_ffn[l]), peer_w_q[l], peer_sub_keys[l], peer_u[l], peer_v[l])
    return rms_norm(h, norm_final)
```

```python
import functools

import jax
import jax.numpy as jnp
import numpy as np
from jax import lax
from jax.experimental import pallas as pl
from jax.experimental.pallas import tpu as pltpu

EPS = 1e-6
LANES = 128
NEG = -1e30

MLA_HEADS = 8
MLA_NOPE = 64
MLA_ROPE = 32
MLA_V = 64
MLA_Q_RANK = 384
MLA_KV_RANK = 256
ROPE_THETA = 10000.0
DIL_HEADS = 8
DIL_HEAD_DIM = 64
DIL_CONFIGS = ((128, 1), (512, 4), (2048, 16))
DIL_WIDTH = DIL_HEADS * DIL_HEAD_DIM
MEM_HEADS = 4
PEER_HEADS = 8
PEER_KEYS = 128
PEER_QDIM = 256
PEER_TOPK = 16

VMEM_LIMIT = 56 * 1024 * 1024

F32 = jnp.float32
BF16 = jnp.bfloat16


def _rms(xf, g):
    ms = jnp.mean(xf * xf, axis=-1, keepdims=True)
    return xf * lax.rsqrt(ms + EPS) * g


def _dot(a, b):
    return jnp.dot(a, b, preferred_element_type=F32)


def _dot_nt(a, b):
    return lax.dot_general(a, b, (((1,), (1,)), ((), ())), preferred_element_type=F32)


def _params(*sem):
    return pltpu.CompilerParams(dimension_semantics=sem, vmem_limit_bytes=VMEM_LIMIT)


N_DQKV = 3 * DIL_WIDTH
O_CQ = N_DQKV
O_CKV = O_CQ + MLA_Q_RANK
O_KR1 = O_CKV + MLA_KV_RANK
O_KR2 = O_KR1 + LANES
N_IN = O_KR2 + LANES
QK_W = MLA_HEADS * LANES


def _inproj_kernel(x_ref, g_ref, w1_ref, qg_ref, kvg_ref, wuq_ref, wukv_ref, tab_ref,
                   dqkv_ref, q_ref, k_ref, v_ref):
    xn = _rms(x_ref[...], g_ref[...]).astype(BF16)
    proj = _dot(xn, w1_ref[...])
    dqkv_ref[...] = proj[:, :N_DQKV]
    cqn = _rms(proj[:, O_CQ:O_CKV], qg_ref[...]).astype(BF16)
    ckvn = _rms(proj[:, O_CKV:O_KR1], kvg_ref[...]).astype(BF16)
    tab = tab_ref[...]
    q2 = _dot(cqn, wuq_ref[...])
    cos_q = jnp.tile(tab[:, 0:LANES], (1, MLA_HEADS))
    sin_q = jnp.tile(tab[:, LANES:2 * LANES], (1, MLA_HEADS))
    q_ref[...] = (q2[:, :QK_W] * cos_q + q2[:, QK_W:] * sin_q).astype(BF16)
    kv = _dot(ckvn, wukv_ref[...])
    kr = (proj[:, O_KR1:O_KR2] * tab[:, 2 * LANES:3 * LANES]
          + proj[:, O_KR2:N_IN] * tab[:, 3 * LANES:4 * LANES])
    k_ref[...] = (kv[:, :QK_W] + jnp.tile(kr, (1, MLA_HEADS))).astype(BF16)
    v_ref[...] = kv[:, QK_W:].astype(BF16)


def _inproj(x2, g_mix, w1, qg, kvg, wuq, wukv, tab, seq, tm):
    T, D = x2.shape
    nblk_seq = seq // tm
    full = lambda a: pl.BlockSpec(a.shape, lambda i: (0, 0))
    return pl.pallas_call(
        _inproj_kernel,
        grid=(T // tm,),
        in_specs=[pl.BlockSpec((tm, D), lambda i: (i, 0)), full(g_mix), full(w1), full(qg),
                  full(kvg), full(wuq), full(wukv),
                  pl.BlockSpec((tm, 4 * LANES), lambda i: (i % nblk_seq, 0))],
        out_specs=[pl.BlockSpec((tm, N_DQKV), lambda i: (i, 0)),
                   pl.BlockSpec((tm, QK_W), lambda i: (i, 0)),
                   pl.BlockSpec((tm, QK_W), lambda i: (i, 0)),
                   pl.BlockSpec((tm, MLA_HEADS * MLA_V), lambda i: (i, 0))],
        out_shape=[jax.ShapeDtypeStruct((T, N_DQKV), F32),
                   jax.ShapeDtypeStruct((T, QK_W), BF16),
                   jax.ShapeDtypeStruct((T, QK_W), BF16),
                   jax.ShapeDtypeStruct((T, MLA_HEADS * MLA_V), BF16)],
        compiler_params=_params("arbitrary"),
        name="inproj",
    )(x2, g_mix, w1, qg, kvg, wuq, wukv, tab)


def _mla_attn_kernel(q_ref, k_ref, v_ref, o_ref, *, tq):
    i = pl.program_id(2)
    outs = []
    for h in range(2):
        q = q_ref[:, h * LANES:(h + 1) * LANES]

        def step(j, carry, diagonal):
            m, l, acc = carry
            start = pl.multiple_of(j * tq, tq)
            kb = k_ref[pl.ds(start, tq), h * LANES:(h + 1) * LANES]
            vb = v_ref[pl.ds(start, tq), :]
            s = _dot_nt(q, kb)
            if diagonal:
                row = lax.broadcasted_iota(jnp.int32, s.shape, 0)
                col = lax.broadcasted_iota(jnp.int32, s.shape, 1)
                s = jnp.where(col <= row, s, NEG)
            m_new = jnp.maximum(m, jnp.max(s, axis=-1, keepdims=True))
            a = jnp.exp(m - m_new)
            p = jnp.exp(s - m_new)
            l = a * l + jnp.sum(p, axis=-1, keepdims=True)
            acc = a * acc + _dot(p.astype(BF16), vb)
            return m_new, l, acc

        init = (jnp.full((tq, 1), NEG, F32), jnp.zeros((tq, 1), F32), jnp.zeros((tq, LANES), F32))
        carry = lax.fori_loop(0, i, functools.partial(step, diagonal=False), init)
        _, l, acc = step(i, carry, True)
        outs.append(acc / l)
    lane = lax.broadcasted_iota(jnp.int32, (tq, LANES), 1)
    o_ref[...] = jnp.where(lane < MLA_V, outs[0], outs[1])


def _mla_attn(q3, k3, v3, tq):
    B, S, _ = q3.shape
    return pl.pallas_call(
        functools.partial(_mla_attn_kernel, tq=tq),
        grid=(B, MLA_HEADS // 2, S // tq),
        in_specs=[pl.BlockSpec((None, tq, 2 * LANES), lambda b, hp, i: (b, i, hp)),
                  pl.BlockSpec((None, S, 2 * LANES), lambda b, hp, i: (b, 0, hp)),
                  pl.BlockSpec((None, S, LANES), lambda b, hp, i: (b, 0, hp))],
        out_specs=pl.BlockSpec((None, tq, LANES), lambda b, hp, i: (b, i, hp)),
        out_shape=jax.ShapeDtypeStruct((B, S, MLA_HEADS * MLA_V), F32),
        compiler_params=_params("arbitrary", "arbitrary", "arbitrary"),
        name="mla_attn",
    )(q3, k3, v3)


DIL_W = 128


def _dil_attn_kernel(q_ref, k_ref, v_ref, o_ref, ob_s, lse_s, *, seq):
    hp = pl.program_id(1)
    W = DIL_W
    lane_q = lax.broadcasted_iota(jnp.int32, (W, LANES), 1)
    qi = lax.broadcasted_iota(jnp.int32, (2 * W, 2 * W), 0) % W
    kj = lax.broadcasted_iota(jnp.int32, (2 * W, 2 * W), 1)
    second = lax.broadcasted_iota(jnp.int32, (2 * W, 2 * W), 0) >= W
    steps = W + qi - kj
    valid = (steps >= 0) & (steps <= W)
    slope0 = jnp.exp2(-(2.0 * hp.astype(F32) + 1.0))
    slope = jnp.where(second, 0.5 * slope0, slope0)
    scale = DIL_HEAD_DIM ** -0.5

    for g, (window, d) in enumerate(DIL_CONFIGS):
        assert window // d == W
        nb = seq // d // W
        bias = jnp.where(valid, -slope * (steps * d).astype(F32), NEG)

        def block(mb, carry, d=d, nb=nb, bias=bias, g=g):
            r = mb // nb
            n = mb % nb
            cur = r + d * W * n
            prev = jnp.where(n > 0, cur - d * W, cur)

            def rows(ref, start):
                if d == 1:
                    return ref[pl.ds(start, W), :]
                return ref[pl.ds(start, W, stride=d), :]

            q = rows(q_ref, cur) * scale
            qs = jnp.concatenate([jnp.where(lane_q < DIL_HEAD_DIM, q, 0.0),
                                  jnp.where(lane_q < DIL_HEAD_DIM, 0.0, q)], axis=0).astype(BF16)
            kb = jnp.concatenate([rows(k_ref, prev), rows(k_ref, cur)], axis=0).astype(BF16)
            vb = jnp.concatenate([rows(v_ref, prev), rows(v_ref, cur)], axis=0).astype(BF16)
            s = _dot_nt(qs, kb) + bias
            s = jnp.where((kj >= W) | (n > 0), s, NEG)
            m = jnp.max(s, axis=-1, keepdims=True)
            p = jnp.exp(s - m)
            l = jnp.sum(p, axis=-1, keepdims=True)
            o = _dot(p.astype(BF16), vb) / l
            lse = m + jnp.log(l)
            o2 = jnp.where(lane_q < DIL_HEAD_DIM, o[:W], o[W:])
            lse2 = jnp.where(lane_q < DIL_HEAD_DIM, lse[:W], lse[W:])
            if d == 1:
                ob_s[g, pl.ds(cur, W), :] = o2
                lse_s[g, pl.ds(cur, W), :] = lse2
            else:
                ob_s[g, pl.ds(cur, W, stride=d), :] = o2
                lse_s[g, pl.ds(cur, W, stride=d), :] = lse2
            return carry

        lax.fori_loop(0, d * nb, block, 0)

    l0, l1, l2 = lse_s[0], lse_s[1], lse_s[2]
    m = jnp.maximum(jnp.maximum(l0, l1), l2)
    w0, w1, w2 = jnp.exp(l0 - m), jnp.exp(l1 - m), jnp.exp(l2 - m)
    o_ref[...] = (w0 * ob_s[0] + w1 * ob_s[1] + w2 * ob_s[2]) / (w0 + w1 + w2)


def _dil_attn(dqkv3):
    B, S, _ = dqkv3.shape
    npair = DIL_HEADS // 2
    return pl.pallas_call(
        functools.partial(_dil_attn_kernel, seq=S),
        grid=(B, npair),
        in_specs=[pl.BlockSpec((None, S, LANES), lambda b, hp: (b, 0, hp)),
                  pl.BlockSpec((None, S, LANES), lambda b, hp: (b, 0, npair + hp)),
                  pl.BlockSpec((None, S, LANES), lambda b, hp: (b, 0, 2 * npair + hp))],
        out_specs=pl.BlockSpec((None, S, LANES), lambda b, hp: (b, 0, hp)),
        out_shape=jax.ShapeDtypeStruct((B, S, DIL_WIDTH), F32),
        scratch_shapes=[pltpu.VMEM((len(DIL_CONFIGS), S, LANES), F32),
                        pltpu.VMEM((len(DIL_CONFIGS), S, LANES), F32)],
        compiler_params=_params("arbitrary", "arbitrary"),
        name="dil_attn",
    )(dqkv3, dqkv3, dqkv3)


def _memkv_kernel(mem_ref, g_ref, w_ref, k_ref, v_ref):
    memn = _rms(mem_ref[...], g_ref[...]).astype(BF16)
    kv = _dot(memn, w_ref[...])
    D = k_ref.shape[-1]
    k_ref[...] = kv[:, :D].astype(BF16)
    v_ref[...] = kv[:, D:].astype(BF16)


def _memkv(mem, g, w):
    B, M, D = mem.shape
    return pl.pallas_call(
        _memkv_kernel,
        grid=(B,),
        in_specs=[pl.BlockSpec((None, M, D), lambda b: (b, 0, 0)),
                  pl.BlockSpec(g.shape, lambda b: (0, 0)),
                  pl.BlockSpec(w.shape, lambda b: (0, 0))],
        out_specs=[pl.BlockSpec((None, M, D), lambda b: (b, 0, 0)),
                   pl.BlockSpec((None, M, D), lambda b: (b, 0, 0))],
        out_shape=[jax.ShapeDtypeStruct((B, M, D), BF16), jax.ShapeDtypeStruct((B, M, D), BF16)],
        compiler_params=_params("arbitrary"),
        name="memkv",
    )(mem, g, w)


def _mixmem_kernel(x_ref, om_ref, od_ref, gm_ref, gd_ref, wo_ref, gq_ref, wq_ref, km_ref, vm_ref,
                   wmo_ref, h_ref):
    half = om_ref.shape[-1]
    mm = _rms(om_ref[...], gm_ref[...]).astype(BF16)
    md = _rms(od_ref[...], gd_ref[...]).astype(BF16)
    h1 = x_ref[...] + _dot(mm, wo_ref[:half, :]) + _dot(md, wo_ref[half:, :])
    hn = _rms(h1, gq_ref[...]).astype(BF16)
    D = h1.shape[-1]
    dh = D // MEM_HEADS
    q = (_dot(hn, wq_ref[...]) * dh ** -0.5).astype(BF16)
    outs = []
    for h in range(MEM_HEADS):
        s = _dot_nt(q[:, h * dh:(h + 1) * dh], km_ref[:, h * dh:(h + 1) * dh])
        m = jnp.max(s, axis=-1, keepdims=True)
        p = jnp.exp(s - m)
        l = jnp.sum(p, axis=-1, keepdims=True)
        outs.append((_dot(p.astype(BF16), vm_ref[:, h * dh:(h + 1) * dh]) / l).astype(BF16))
    o = jnp.concatenate(outs, axis=-1)
    h_ref[...] = h1 + _dot(o, wmo_ref[...])


def _mixmem(x2, om, od, gm, gd, wo, gq, wq, km, vm, wmo, seq, tm):
    T, D = x2.shape
    M = km.shape[1]
    nblk_seq = seq // tm
    full = lambda a: pl.BlockSpec(a.shape, lambda i: (0, 0))
    return pl.pallas_call(
        _mixmem_kernel,
        grid=(T // tm,),
        in_specs=[pl.BlockSpec((tm, D), lambda i: (i, 0)),
                  pl.BlockSpec((tm, om.shape[1]), lambda i: (i, 0)),
                  pl.BlockSpec((tm, od.shape[1]), lambda i: (i, 0)),
                  full(gm), full(gd), full(wo), full(gq), full(wq),
                  pl.BlockSpec((None, M, D), lambda i: (i // nblk_seq, 0, 0)),
                  pl.BlockSpec((None, M, D), lambda i: (i // nblk_seq, 0, 0)),
                  full(wmo)],
        out_specs=pl.BlockSpec((tm, D), lambda i: (i, 0)),
        out_shape=jax.ShapeDtypeStruct((T, D), F32),
        compiler_params=_params("arbitrary"),
        name="mixmem",
    )(x2, om, od, gm, gd, wo, gq, wq, km, vm, wmo)


def _top_values(s, k):
    vals = []
    for _ in range(k):
        mx = jnp.max(s, axis=0, keepdims=True)
        vals.append(mx)
        s = jnp.where(s == mx, -jnp.inf, s)
    return vals


_CAND_PAIRS = [(k, l) for k in range(PEER_TOPK) for l in range(PEER_TOPK)
               if (k + 1) * (l + 1) <= PEER_TOPK]


def _peer_kernel(h_ref, gf_ref, wqt_ref, keys_ref, u_ref, vt_ref, gfin_ref, out_ref,
                 xnt_s, rank_s, e1_s, cnt_s, e0_s, acc_s, *, eb, final):
    e = pl.program_id(1)
    half = PEER_QDIM // 2
    nk = PEER_KEYS

    @pl.when(e == 0)
    def _prologue():
        xn = _rms(h_ref[...], gf_ref[...])
        xnt = xn.T.astype(BF16)
        xnt_s[...] = xnt
        acc_s[...] = jnp.zeros_like(acc_s)
        for hh in range(PEER_HEADS):
            qt = _dot(wqt_ref[hh * PEER_QDIM:(hh + 1) * PEER_QDIM, :], xnt)
            s0 = _dot(keys_ref[hh, 0], qt[:half].astype(BF16))
            s1 = _dot(keys_ref[hh, 1], qt[half:].astype(BF16))
            a = _top_values(s0, PEER_TOPK)
            b = _top_values(s1, PEER_TOPK)
            cands = [a[k] + b[l] for (k, l) in _CAND_PAIRS]
            pad = (-len(cands)) % 8
            cands += [jnp.full_like(cands[0], -jnp.inf)] * pad
            top = _top_values(jnp.concatenate(cands, axis=0), PEER_TOPK)
            z = sum(jnp.exp(t - top[0]) for t in top)
            tau = top[PEER_TOPK - 1]
            cnt = jnp.zeros_like(s0)
            rank = jnp.zeros_like(s1)
            for r in range(PEER_TOPK):
                cnt = cnt + jnp.where(s0 + b[r] >= tau, 1.0, 0.0)
                rank = rank + jnp.where(b[r] > s1, 1.0, 0.0)
            cnt_s[hh] = cnt
            e0_s[hh] = jnp.exp(s0 - a[0]) / z
            rank_s[hh] = rank
            e1_s[hh] = jnp.exp(s1 - b[0])

    at = _dot(u_ref[...], xnt_s[...])
    gl = 0.5 * at * (1.0 + lax.erf(at * np.float32(np.sqrt(0.5))))
    rows = []
    for ii in range(eb // nk):
        i = e * (eb // nk) + ii
        c = None
        for hh in range(PEER_HEADS):
            cnt = cnt_s[hh, pl.ds(i, 1), :]
            e0 = e0_s[hh, pl.ds(i, 1), :]
            term = e0 * jnp.where(rank_s[hh] < cnt, e1_s[hh], 0.0)
            c = term if c is None else c + term
        rows.append(c)
    ct = (jnp.concatenate(rows, axis=0) * gl).astype(BF16)
    acc_s[...] += _dot(vt_ref[...], ct)

    @pl.when(e == pl.num_programs(1) - 1)
    def _epilogue():
        y = h_ref[...] + acc_s[...].T
        out_ref[...] = _rms(y, gfin_ref[...]) if final else y


def _peer(h2, gf, wqt, keys, u, vt, gfin, tt, eb, final):
    T, D = h2.shape
    n_exp = u.shape[0]
    return pl.pallas_call(
        functools.partial(_peer_kernel, eb=eb, final=final),
        grid=(T // tt, n_exp // eb),
        in_specs=[pl.BlockSpec((tt, D), lambda t, e: (t, 0)),
                  pl.BlockSpec(gf.shape, lambda t, e: (0, 0)),
                  pl.BlockSpec(wqt.shape, lambda t, e: (0, 0)),
                  pl.BlockSpec(keys.shape, lambda t, e: (0, 0, 0, 0)),
                  pl.BlockSpec((eb, D), lambda t, e: (e, 0)),
                  pl.BlockSpec((D, eb), lambda t, e: (0, e)),
                  pl.BlockSpec(gfin.shape, lambda t, e: (0, 0))],
        out_specs=pl.BlockSpec((tt, D), lambda t, e: (t, 0)),
        out_shape=jax.ShapeDtypeStruct((T, D), F32),
        scratch_shapes=[pltpu.VMEM((D, tt), BF16),
                        pltpu.VMEM((PEER_HEADS, PEER_KEYS, tt), F32),
                        pltpu.VMEM((PEER_HEADS, PEER_KEYS, tt), F32),
                        pltpu.VMEM((PEER_HEADS, PEER_KEYS, tt), F32),
                        pltpu.VMEM((PEER_HEADS, PEER_KEYS, tt), F32),
                        pltpu.VMEM((D, tt), F32)],
        compiler_params=_params("arbitrary", "arbitrary"),
        name="peer",
    )(h2, gf, wqt, keys, u, vt, gfin)


def _rope_tables(seq):
    half = MLA_ROPE // 2
    inv = 1.0 / (ROPE_THETA ** (jnp.arange(0, MLA_ROPE, 2, dtype=F32) / MLA_ROPE))
    ang = jnp.arange(seq, dtype=F32)[:, None] * inv[None, :]
    cos, sin = jnp.cos(ang), jnp.sin(ang)
    scale = (MLA_NOPE + MLA_ROPE) ** -0.5
    ones = jnp.ones((seq, MLA_NOPE), F32)
    z_nope = jnp.zeros((seq, MLA_NOPE), F32)
    z_pad = jnp.zeros((seq, LANES - MLA_NOPE - MLA_ROPE), F32)
    cos_q = jnp.concatenate([ones, cos, cos, z_pad], axis=1) * scale
    sin_q = jnp.concatenate([z_nope, sin, sin, z_pad], axis=1) * scale
    cos_k = jnp.concatenate([z_nope, cos, cos, z_pad], axis=1)
    sin_k = jnp.concatenate([z_nope, sin, sin, z_pad], axis=1)
    del half
    return jnp.concatenate([cos_q, sin_q, cos_k, sin_k], axis=1)


def _rot_half(w):
    h = w.shape[-1] // 2
    return jnp.concatenate([-w[..., h:], w[..., :h]], axis=-1)


def _prep_weights(w_in, w_uq, w_ukv):
    D = w_in.shape[0]
    o1 = MLA_Q_RANK
    o2 = o1 + MLA_KV_RANK
    o3 = o2 + MLA_ROPE
    w_cq, w_ckv, w_kr, w_dqkv = w_in[:, :o1], w_in[:, o1:o2], w_in[:, o2:o3], w_in[:, o3:]
    lead = jnp.zeros((D, MLA_NOPE), F32)
    tail = jnp.zeros((D, LANES - MLA_NOPE - MLA_ROPE), F32)
    kr1 = jnp.concatenate([lead, w_kr, tail], axis=1)
    kr2 = jnp.concatenate([lead, _rot_half(w_kr), tail], axis=1)
    w1 = jnp.concatenate([w_dqkv, w_cq, w_ckv, kr1, kr2], axis=1).astype(BF16)

    dqk = MLA_NOPE + MLA_ROPE
    wq = w_uq.reshape(MLA_Q_RANK, MLA_HEADS, dqk)
    zq_pad = jnp.zeros((MLA_Q_RANK, MLA_HEADS, LANES - dqk), F32)
    zq_nope = jnp.zeros((MLA_Q_RANK, MLA_HEADS, MLA_NOPE), F32)
    q_plain = jnp.concatenate([wq, zq_pad], axis=2).reshape(MLA_Q_RANK, QK_W)
    q_rot = jnp.concatenate([zq_nope, _rot_half(wq[:, :, MLA_NOPE:]), zq_pad], axis=2).reshape(MLA_Q_RANK, QK_W)
    wuq = jnp.concatenate([q_plain, q_rot], axis=1).astype(BF16)

    wkv = w_ukv.reshape(MLA_KV_RANK, MLA_HEADS, MLA_NOPE + MLA_V)
    zk = jnp.zeros((MLA_KV_RANK, MLA_HEADS, LANES - MLA_NOPE), F32)
    k_pad = jnp.concatenate([wkv[:, :, :MLA_NOPE], zk], axis=2).reshape(MLA_KV_RANK, QK_W)
    v_cols = wkv[:, :, MLA_NOPE:].reshape(MLA_KV_RANK, MLA_HEADS * MLA_V)
    wukv = jnp.concatenate([k_pad, v_cols], axis=1).astype(BF16)
    return w1, wuq, wukv


def _row(v):
    return v.reshape(1, -1)


def kernel(x, mem, norm_mix, w_in, mla_q_norm, mla_kv_norm, mla_w_uq, mla_w_ukv, grp_norm_mla,
           grp_norm_dil, w_out, norm_mem_q, norm_mem_kv, w_mem_q, w_mem_kv, w_mem_o, norm_ffn,
           peer_w_q, peer_sub_keys, peer_u, peer_v, norm_final):
    B, S, D = x.shape
    T = B * S
    tm = min(512, S)
    depth = norm_mix.shape[0]
    tab = _rope_tables(S)
    h = x.reshape(T, D)
    for l in range(depth):
        w1, wuq, wukv = _prep_weights(w_in[l], mla_w_uq[l], mla_w_ukv[l])
        dqkv, q, k, v = _inproj(h, _row(norm_mix[l]), w1, _row(mla_q_norm[l]), _row(mla_kv_norm[l]),
                                wuq, wukv, tab, S, tm)
        o_mla = _mla_attn(q.reshape(B, S, -1), k.reshape(B, S, -1), v.reshape(B, S, -1), tm)
        o_dil = _dil_attn(dqkv.reshape(B, S, -1))
        km, vm = _memkv(mem, _row(norm_mem_kv[l]), w_mem_kv[l].astype(BF16))
        h2 = _mixmem(h, o_mla.reshape(T, -1), o_dil.reshape(T, -1), _row(grp_norm_mla[l]),
                     _row(grp_norm_dil[l]), w_out[l].astype(BF16), _row(norm_mem_q[l]),
                     w_mem_q[l].astype(BF16), km, vm, w_mem_o[l].astype(BF16), S, tm)
        h = _peer(h2, _row(norm_ffn[l]), peer_w_q[l].T.astype(BF16), peer_sub_keys[l].astype(BF16),
                  peer_u[l].astype(BF16), peer_v[l].T.astype(BF16), _row(norm_final), tm, 512,
                  final=l == depth - 1)
    return h.reshape(B, S, D)
```

```python
import functools

import jax
import jax.numpy as jnp
import numpy as np
from jax import lax
from jax.experimental import pallas as pl
from jax.experimental.pallas import tpu as pltpu

EPS = 1e-6
LANES = 128
NEG = -1e30

MLA_HEADS = 8
MLA_NOPE = 64
MLA_ROPE = 32
MLA_V = 64
MLA_Q_RANK = 384
MLA_KV_RANK = 256
ROPE_THETA = 10000.0
DIL_HEADS = 8
DIL_HEAD_DIM = 64
DIL_CONFIGS = ((128, 1), (512, 4), (2048, 16))
DIL_WIDTH = DIL_HEADS * DIL_HEAD_DIM
MEM_HEADS = 4
PEER_HEADS = 8
PEER_KEYS = 128
PEER_QDIM = 256
PEER_TOPK = 16

VMEM_LIMIT = 56 * 1024 * 1024

F32 = jnp.float32
BF16 = jnp.bfloat16


def _rms(xf, g):
    ms = jnp.mean(xf * xf, axis=-1, keepdims=True)
    return xf * lax.rsqrt(ms + EPS) * g


def _dot(a, b):
    return jnp.dot(a, b, preferred_element_type=F32)


def _dot_nt(a, b):
    return lax.dot_general(a, b, (((1,), (1,)), ((), ())), preferred_element_type=F32)


def _params(*sem):
    return pltpu.CompilerParams(dimension_semantics=sem, vmem_limit_bytes=VMEM_LIMIT)


N_DQKV = 3 * DIL_WIDTH
O_CQ = N_DQKV
O_CKV = O_CQ + MLA_Q_RANK
O_KR1 = O_CKV + MLA_KV_RANK
O_KR2 = O_KR1 + LANES
N_IN = O_KR2 + LANES
QK_W = MLA_HEADS * LANES


def _inproj_kernel(x_ref, g_ref, w1_ref, qg_ref, kvg_ref, wuq_ref, wukv_ref, tab_ref,
                   dqkv_ref, q_ref, k_ref, v_ref):
    xn = _rms(x_ref[...], g_ref[...]).astype(BF16)
    proj = _dot(xn, w1_ref[...])
    dqkv_ref[...] = proj[:, :N_DQKV]
    cqn = _rms(proj[:, O_CQ:O_CKV], qg_ref[...]).astype(BF16)
    ckvn = _rms(proj[:, O_CKV:O_KR1], kvg_ref[...]).astype(BF16)
    tab = tab_ref[...]
    q2 = _dot(cqn, wuq_ref[...])
    cos_q = jnp.tile(tab[:, 0:LANES], (1, MLA_HEADS))
    sin_q = jnp.tile(tab[:, LANES:2 * LANES], (1, MLA_HEADS))
    q_ref[...] = (q2[:, :QK_W] * cos_q + q2[:, QK_W:] * sin_q).astype(BF16)
    kv = _dot(ckvn, wukv_ref[...])
    kr = (proj[:, O_KR1:O_KR2] * tab[:, 2 * LANES:3 * LANES]
          + proj[:, O_KR2:N_IN] * tab[:, 3 * LANES:4 * LANES])
    k_ref[...] = (kv[:, :QK_W] + jnp.tile(kr, (1, MLA_HEADS))).astype(BF16)
    v_ref[...] = kv[:, QK_W:].astype(BF16)


def _inproj(x2, g_mix, w1, qg, kvg, wuq, wukv, tab, seq, tm):
    T, D = x2.shape
    nblk_seq = seq // tm
    full = lambda a: pl.BlockSpec(a.shape, lambda i: (0, 0))
    return pl.pallas_call(
        _inproj_kernel,
        grid=(T // tm,),
        in_specs=[pl.BlockSpec((tm, D), lambda i: (i, 0)), full(g_mix), full(w1), full(qg),
                  full(kvg), full(wuq), full(wukv),
                  pl.BlockSpec((tm, 4 * LANES), lambda i: (i % nblk_seq, 0))],
        out_specs=[pl.BlockSpec((tm, N_DQKV), lambda i: (i, 0)),
                   pl.BlockSpec((tm, QK_W), lambda i: (i, 0)),
                   pl.BlockSpec((tm, QK_W), lambda i: (i, 0)),
                   pl.BlockSpec((tm, MLA_HEADS * MLA_V), lambda i: (i, 0))],
        out_shape=[jax.ShapeDtypeStruct((T, N_DQKV), F32),
                   jax.ShapeDtypeStruct((T, QK_W), BF16),
                   jax.ShapeDtypeStruct((T, QK_W), BF16),
                   jax.ShapeDtypeStruct((T, MLA_HEADS * MLA_V), BF16)],
        compiler_params=_params("arbitrary"),
        name="inproj",
    )(x2, g_mix, w1, qg, kvg, wuq, wukv, tab)


def _mla_attn_kernel(q_ref, k_ref, v_ref, o_ref, *, tq):
    i = pl.program_id(2)
    outs = []
    for h in range(2):
        q = q_ref[:, h * LANES:(h + 1) * LANES]

        def step(j, carry, diagonal):
            m, l, acc = carry
            start = pl.multiple_of(j * tq, tq)
            kb = k_ref[pl.ds(start, tq), h * LANES:(h + 1) * LANES]
            vb = v_ref[pl.ds(start, tq), :]
            s = _dot_nt(q, kb)
            if diagonal:
                row = lax.broadcasted_iota(jnp.int32, s.shape, 0)
                col = lax.broadcasted_iota(jnp.int32, s.shape, 1)
                s = jnp.where(col <= row, s, NEG)
            m_new = jnp.maximum(m, jnp.max(s, axis=-1, keepdims=True))
            a = jnp.exp(m - m_new)
            p = jnp.exp(s - m_new)
            l = a * l + jnp.sum(p, axis=-1, keepdims=True)
            acc = a * acc + _dot(p.astype(BF16), vb)
            return m_new, l, acc

        init = (jnp.full((tq, 1), NEG, F32), jnp.zeros((tq, 1), F32), jnp.zeros((tq, LANES), F32))
        carry = lax.fori_loop(0, i, functools.partial(step, diagonal=False), init)
        _, l, acc = step(i, carry, True)
        outs.append(acc / l)
    lane = lax.broadcasted_iota(jnp.int32, (tq, LANES), 1)
    o_ref[...] = jnp.where(lane < MLA_V, outs[0], outs[1])


def _mla_attn(q3, k3, v3, tq):
    B, S, _ = q3.shape
    return pl.pallas_call(
        functools.partial(_mla_attn_kernel, tq=tq),
        grid=(B, MLA_HEADS // 2, S // tq),
        in_specs=[pl.BlockSpec((None, tq, 2 * LANES), lambda b, hp, i: (b, i, hp)),
                  pl.BlockSpec((None, S, 2 * LANES), lambda b, hp, i: (b, 0, hp)),
                  pl.BlockSpec((None, S, LANES), lambda b, hp, i: (b, 0, hp))],
        out_specs=pl.BlockSpec((None, tq, LANES), lambda b, hp, i: (b, i, hp)),
        out_shape=jax.ShapeDtypeStruct((B, S, MLA_HEADS * MLA_V), F32),
        compiler_params=_params("arbitrary", "arbitrary", "arbitrary"),
        name="mla_attn",
    )(q3, k3, v3)


DIL_W = 128


def _dil_attn_kernel(q_ref, k_ref, v_ref, o_ref, ob_s, lse_s, *, seq):
    hp = pl.program_id(1)
    W = DIL_W
    lane_q = lax.broadcasted_iota(jnp.int32, (W, LANES), 1)
    qi = lax.broadcasted_iota(jnp.int32, (2 * W, 2 * W), 0) % W
    kj = lax.broadcasted_iota(jnp.int32, (2 * W, 2 * W), 1)
    second = lax.broadcasted_iota(jnp.int32, (2 * W, 2 * W), 0) >= W
    steps = W + qi - kj
    valid = (steps >= 0) & (steps <= W)
    slope0 = jnp.exp2(-(2.0 * hp.astype(F32) + 1.0))
    slope = jnp.where(second, 0.5 * slope0, slope0)
    scale = DIL_HEAD_DIM ** -0.5

    for g, (window, d) in enumerate(DIL_CONFIGS):
        assert window // d == W
        nb = seq // d // W
        bias = jnp.where(valid, -slope * (steps * d).astype(F32), NEG)

        def block(mb, carry, d=d, nb=nb, bias=bias, g=g):
            r = mb // nb
            n = mb % nb
            cur = r + d * W * n
            prev = jnp.where(n > 0, cur - d * W, cur)

            def rows(ref, start):
                if d == 1:
                    return ref[pl.ds(start, W), :]
                return ref[pl.ds(start, W, stride=d), :]

            q = rows(q_ref, cur) * scale
            qs = jnp.concatenate([jnp.where(lane_q < DIL_HEAD_DIM, q, 0.0),
                                  jnp.where(lane_q < DIL_HEAD_DIM, 0.0, q)], axis=0).astype(BF16)
            kb = jnp.concatenate([rows(k_ref, prev), rows(k_ref, cur)], axis=0).astype(BF16)
            vb = jnp.concatenate([rows(v_ref, prev), rows(v_ref, cur)], axis=0).astype(BF16)
            s = _dot_nt(qs, kb) + bias
            s = jnp.where((kj >= W) | (n > 0), s, NEG)
            m = jnp.max(s, axis=-1, keepdims=True)
            p = jnp.exp(s - m)
            l = jnp.sum(p, axis=-1, keepdims=True)
            o = _dot(p.astype(BF16), vb) / l
            lse = m + jnp.log(l)
            o2 = jnp.where(lane_q < DIL_HEAD_DIM, o[:W], o[W:])
            lse2 = jnp.where(lane_q < DIL_HEAD_DIM, lse[:W], lse[W:])
            if d == 1:
                ob_s[g, pl.ds(cur, W), :] = o2
                lse_s[g, pl.ds(cur, W), :] = lse2
            else:
                ob_s[g, pl.ds(cur, W, stride=d), :] = o2
                lse_s[g, pl.ds(cur, W, stride=d), :] = lse2
            return carry

        lax.fori_loop(0, d * nb, block, 0)

    l0, l1, l2 = lse_s[0], lse_s[1], lse_s[2]
    m = jnp.maximum(jnp.maximum(l0, l1), l2)
    w0, w1, w2 = jnp.exp(l0 - m), jnp.exp(l1 - m), jnp.exp(l2 - m)
    o_ref[...] = (w0 * ob_s[0] + w1 * ob_s[1] + w2 * ob_s[2]) / (w0 + w1 + w2)


def _dil_attn(dqkv3):
    B, S, _ = dqkv3.shape
    npair = DIL_HEADS // 2
    return pl.pallas_call(
        functools.partial(_dil_attn_kernel, seq=S),
        grid=(B, npair),
        in_specs=[pl.BlockSpec((None, S, LANES), lambda b, hp: (b, 0, hp)),
                  pl.BlockSpec((None, S, LANES), lambda b, hp: (b, 0, npair + hp)),
                  pl.BlockSpec((None, S, LANES), lambda b, hp: (b, 0, 2 * npair + hp))],
        out_specs=pl.BlockSpec((None, S, LANES), lambda b, hp: (b, 0, hp)),
        out_shape=jax.ShapeDtypeStruct((B, S, DIL_WIDTH), F32),
        scratch_shapes=[pltpu.VMEM((len(DIL_CONFIGS), S, LANES), F32),
                        pltpu.VMEM((len(DIL_CONFIGS), S, LANES), F32)],
        compiler_params=_params("arbitrary", "arbitrary"),
        name="dil_attn",
    )(dqkv3, dqkv3, dqkv3)


def _memkv_kernel(mem_ref, g_ref, w_ref, k_ref, v_ref):
    memn = _rms(mem_ref[...], g_ref[...]).astype(BF16)
    kv = _dot(memn, w_ref[...])
    D = k_ref.shape[-1]
    k_ref[...] = kv[:, :D].astype(BF16)
    v_ref[...] = kv[:, D:].astype(BF16)


def _memkv(mem, g, w):
    B, M, D = mem.shape
    return pl.pallas_call(
        _memkv_kernel,
        grid=(B,),
        in_specs=[pl.BlockSpec((None, M, D), lambda b: (b, 0, 0)),
                  pl.BlockSpec(g.shape, lambda b: (0, 0)),
                  pl.BlockSpec(w.shape, lambda b: (0, 0))],
        out_specs=[pl.BlockSpec((None, M, D), lambda b: (b, 0, 0)),
                   pl.BlockSpec((None, M, D), lambda b: (b, 0, 0))],
        out_shape=[jax.ShapeDtypeStruct((B, M, D), BF16), jax.ShapeDtypeStruct((B, M, D), BF16)],
        compiler_params=_params("arbitrary"),
        name="memkv",
    )(mem, g, w)


def _mixmem_kernel(x_ref, om_ref, od_ref, gm_ref, gd_ref, wo_ref, gq_ref, wq_ref, km_ref, vm_ref,
                   wmo_ref, h_ref):
    half = om_ref.shape[-1]
    mm = _rms(om_ref[...], gm_ref[...]).astype(BF16)
    md = _rms(od_ref[...], gd_ref[...]).astype(BF16)
    h1 = x_ref[...] + _dot(mm, wo_ref[:half, :]) + _dot(md, wo_ref[half:, :])
    hn = _rms(h1, gq_ref[...]).astype(BF16)
    D = h1.shape[-1]
    dh = D // MEM_HEADS
    q = (_dot(hn, wq_ref[...]) * dh ** -0.5).astype(BF16)
    outs = []
    for h in range(MEM_HEADS):
        s = _dot_nt(q[:, h * dh:(h + 1) * dh], km_ref[:, h * dh:(h + 1) * dh])
        m = jnp.max(s, axis=-1, keepdims=True)
        p = jnp.exp(s - m)
        l = jnp.sum(p, axis=-1, keepdims=True)
        outs.append((_dot(p.astype(BF16), vm_ref[:, h * dh:(h + 1) * dh]) / l).astype(BF16))
    o = jnp.concatenate(outs, axis=-1)
    h_ref[...] = h1 + _dot(o, wmo_ref[...])


def _mixmem(x2, om, od, gm, gd, wo, gq, wq, km, vm, wmo, seq, tm):
    T, D = x2.shape
    M = km.shape[1]
    nblk_seq = seq // tm
    full = lambda a: pl.BlockSpec(a.shape, lambda i: (0, 0))
    return pl.pallas_call(
        _mixmem_kernel,
        grid=(T // tm,),
        in_specs=[pl.BlockSpec((tm, D), lambda i: (i, 0)),
                  pl.BlockSpec((tm, om.shape[1]), lambda i: (i, 0)),
                  pl.BlockSpec((tm, od.shape[1]), lambda i: (i, 0)),
                  full(gm), full(gd), full(wo), full(gq), full(wq),
                  pl.BlockSpec((None, M, D), lambda i: (i // nblk_seq, 0, 0)),
                  pl.BlockSpec((None, M, D), lambda i: (i // nblk_seq, 0, 0)),
                  full(wmo)],
        out_specs=pl.BlockSpec((tm, D), lambda i: (i, 0)),
        out_shape=jax.ShapeDtypeStruct((T, D), F32),
        compiler_params=_params("arbitrary"),
        name="mixmem",
    )(x2, om, od, gm, gd, wo, gq, wq, km, vm, wmo)


SUBLANES = 8
BF16_ROWS = 16


def _batcher_pairs(n):
    pairs, p = [], 1
    while p < n:
        k = p
        while k >= 1:
            for j in range(k % p, n - k, 2 * k):
                for i in range(min(k, n - j - k)):
                    if (i + j) // (2 * p) == (i + j + k) // (2 * p):
                        pairs.append((i + j, i + j + k))
            k //= 2
        p *= 2
    return pairs


def _bitonic_pairs(n):
    pairs, s = [], n // 2
    while s >= 1:
        pairs += [(i, i + s) for i in range(n) if i & s == 0]
        s //= 2
    return pairs


def _apply_desc(v, pairs):
    for i, j in pairs:
        v[i], v[j] = jnp.maximum(v[i], v[j]), jnp.minimum(v[i], v[j])
    return v


def _merge_sublanes(v, shifts):
    n = len(v)
    for shift in shifts:
        w = [pltpu.roll(x, shift, axis=0) for x in v]
        v = [jnp.maximum(v[k], w[n - 1 - k]) for k in range(n)]
        v = _apply_desc(v, _bitonic_pairs(n))
    return v


def _top16_of_keys(s):
    v = [s[SUBLANES * k:SUBLANES * (k + 1)] for k in range(s.shape[0] // SUBLANES)]
    assert len(v) == PEER_TOPK
    v = _apply_desc(v, _batcher_pairs(PEER_TOPK))
    return _merge_sublanes(v, (4, 2, 1))


_CAND_PAIRS = [(k, l) for k in range(PEER_TOPK) for l in range(PEER_TOPK)
               if (k + 1) * (l + 1) <= PEER_TOPK]


def _top16_of_pair_sums(a, b):
    cands = [a[k] + b[l] for (k, l) in _CAND_PAIRS]
    sub = lax.broadcasted_iota(jnp.int32, a[0].shape, 0)
    packed = []
    for m in range(SUBLANES):
        x = jnp.full(a[0].shape, -jnp.inf, F32)
        for s_ in range(SUBLANES):
            idx = SUBLANES * m + s_
            if idx < len(cands):
                x = jnp.where(sub == s_, cands[idx], x)
        packed.append(x)
    assert len(cands) <= SUBLANES * SUBLANES
    packed = _apply_desc(packed, _batcher_pairs(SUBLANES))
    w = [pltpu.roll(x, 4, axis=0) for x in packed]
    v = _apply_desc(packed + w[::-1], _bitonic_pairs(2 * SUBLANES))
    return _merge_sublanes(v, (2, 1))


def _peer_kernel(h_ref, gf_ref, wqt_ref, keys_ref, u_ref, vt_ref, gfin_ref, out_ref,
                 xnt_s, rank_s, e1_s, cnt_s, e0_s, acc_s, gate_a, gate_b, *, eb, final):
    e = pl.program_id(1)
    n_e = pl.num_programs(1)
    half = PEER_QDIM // 2
    nk = PEER_KEYS
    tt = xnt_s.shape[-1]
    sub_blocks = eb // nk
    zero = jnp.zeros((), BF16)

    def row16(ref, hh, i):
        return jnp.broadcast_to(ref[hh, pl.ds(i, 1), :], (BF16_ROWS, tt)).astype(BF16)[None]

    def gate_block(blk):
        rows = []
        for ii in range(sub_blocks):
            i = blk * sub_blocks + ii
            c = None
            for hh in range(PEER_HEADS):
                term = row16(e0_s, hh, i) * jnp.where(rank_s[hh] < row16(cnt_s, hh, i), e1_s[hh], zero)
                c = term if c is None else c + term
            rows.append(c.reshape(nk, tt))
        return jnp.concatenate(rows, axis=0)

    def apply_block(u_blk, vt_blk, gate):
        at = _dot(u_blk, xnt_s[...])
        gl = 0.5 * at * (1.0 + lax.erf(at * np.float32(np.sqrt(0.5))))
        acc_s[...] += _dot(vt_blk, gate * gl.astype(BF16))

    @pl.when(e == 0)
    def _prologue():
        xn = _rms(h_ref[...], gf_ref[...])
        xnt = xn.T.astype(BF16)
        xnt_s[...] = xnt
        acc_s[...] = jnp.zeros_like(acc_s)
        for hh in range(PEER_HEADS):
            qt = _dot(wqt_ref[hh * PEER_QDIM:(hh + 1) * PEER_QDIM, :], xnt)
            s0 = _dot(keys_ref[hh, 0], qt[:half].astype(BF16))
            s1 = _dot(keys_ref[hh, 1], qt[half:].astype(BF16))
            a = _top16_of_keys(s0)
            b = _top16_of_keys(s1)
            top = _top16_of_pair_sums(a, b)
            z = sum(jnp.exp(t - top[0]) for t in top)
            tau = top[PEER_TOPK - 1]
            s0 = s0.reshape(nk // SUBLANES, SUBLANES, tt)
            s1 = s1.reshape(nk // SUBLANES, SUBLANES, tt)
            cnt = jnp.zeros_like(s0)
            rank = jnp.zeros_like(s1)
            for r in range(PEER_TOPK):
                cnt = jnp.where(s0 + b[r][None] >= tau[None], r + 1.0, cnt)
                rank = jnp.where(b[r][None] > s1, r + 1.0, rank)
            cnt_s[hh] = cnt.reshape(nk, tt)
            e0_s[hh] = (jnp.exp(s0 - a[0][None]) / z[None]).reshape(nk, tt)
            rank_s[hh] = rank.reshape(nk // BF16_ROWS, BF16_ROWS, tt).astype(BF16)
            e1_s[hh] = jnp.exp(s1 - b[0][None]).reshape(nk // BF16_ROWS, BF16_ROWS, tt).astype(BF16)

        gate_a[...] = gate_block(0)

    gate_b[...] = gate_block(2 * e + 1)
    apply_block(u_ref[:eb, :], vt_ref[:, :eb], gate_a[...])
    gate_a[...] = gate_block(jnp.minimum(2 * e + 2, 2 * n_e - 1))
    apply_block(u_ref[eb:, :], vt_ref[:, eb:], gate_b[...])

    @pl.when(e == n_e - 1)
    def _epilogue():
        y = h_ref[...] + acc_s[...].T
        out_ref[...] = _rms(y, gfin_ref[...]) if final else y


def _peer(h2, gf, wqt, keys, u, vt, gfin, tt, eb, final):
    T, D = h2.shape
    n_exp = u.shape[0]
    return pl.pallas_call(
        functools.partial(_peer_kernel, eb=eb, final=final),
        grid=(T // tt, n_exp // (2 * eb)),
        in_specs=[pl.BlockSpec((tt, D), lambda t, e: (t, 0)),
                  pl.BlockSpec(gf.shape, lambda t, e: (0, 0)),
                  pl.BlockSpec(wqt.shape, lambda t, e: (0, 0)),
                  pl.BlockSpec(keys.shape, lambda t, e: (0, 0, 0, 0)),
                  pl.BlockSpec((2 * eb, D), lambda t, e: (e, 0)),
                  pl.BlockSpec((D, 2 * eb), lambda t, e: (0, e)),
                  pl.BlockSpec(gfin.shape, lambda t, e: (0, 0))],
        out_specs=pl.BlockSpec((tt, D), lambda t, e: (t, 0)),
        out_shape=jax.ShapeDtypeStruct((T, D), F32),
        scratch_shapes=[pltpu.VMEM((D, tt), BF16),
                        pltpu.VMEM((PEER_HEADS, PEER_KEYS // BF16_ROWS, BF16_ROWS, tt), BF16),
                        pltpu.VMEM((PEER_HEADS, PEER_KEYS // BF16_ROWS, BF16_ROWS, tt), BF16),
                        pltpu.VMEM((PEER_HEADS, PEER_KEYS, tt), F32),
                        pltpu.VMEM((PEER_HEADS, PEER_KEYS, tt), F32),
                        pltpu.VMEM((D, tt), F32),
                        pltpu.VMEM((eb, tt), BF16),
                        pltpu.VMEM((eb, tt), BF16)],
        compiler_params=_params("arbitrary", "arbitrary"),
        name="peer",
    )(h2, gf, wqt, keys, u, vt, gfin)


def _rope_tables(seq):
    half = MLA_ROPE // 2
    inv = 1.0 / (ROPE_THETA ** (jnp.arange(0, MLA_ROPE, 2, dtype=F32) / MLA_ROPE))
    ang = jnp.arange(seq, dtype=F32)[:, None] * inv[None, :]
    cos, sin = jnp.cos(ang), jnp.sin(ang)
    scale = (MLA_NOPE + MLA_ROPE) ** -0.5
    ones = jnp.ones((seq, MLA_NOPE), F32)
    z_nope = jnp.zeros((seq, MLA_NOPE), F32)
    z_pad = jnp.zeros((seq, LANES - MLA_NOPE - MLA_ROPE), F32)
    cos_q = jnp.concatenate([ones, cos, cos, z_pad], axis=1) * scale
    sin_q = jnp.concatenate([z_nope, sin, sin, z_pad], axis=1) * scale
    cos_k = jnp.concatenate([z_nope, cos, cos, z_pad], axis=1)
    sin_k = jnp.concatenate([z_nope, sin, sin, z_pad], axis=1)
    del half
    return jnp.concatenate([cos_q, sin_q, cos_k, sin_k], axis=1)


def _rot_half(w):
    h = w.shape[-1] // 2
    return jnp.concatenate([-w[..., h:], w[..., :h]], axis=-1)


def _prep_weights(w_in, w_uq, w_ukv):
    D = w_in.shape[0]
    o1 = MLA_Q_RANK
    o2 = o1 + MLA_KV_RANK
    o3 = o2 + MLA_ROPE
    w_cq, w_ckv, w_kr, w_dqkv = w_in[:, :o1], w_in[:, o1:o2], w_in[:, o2:o3], w_in[:, o3:]
    lead = jnp.zeros((D, MLA_NOPE), F32)
    tail = jnp.zeros((D, LANES - MLA_NOPE - MLA_ROPE), F32)
    kr1 = jnp.concatenate([lead, w_kr, tail], axis=1)
    kr2 = jnp.concatenate([lead, _rot_half(w_kr), tail], axis=1)
    w1 = jnp.concatenate([w_dqkv, w_cq, w_ckv, kr1, kr2], axis=1).astype(BF16)

    dqk = MLA_NOPE + MLA_ROPE
    wq = w_uq.reshape(MLA_Q_RANK, MLA_HEADS, dqk)
    zq_pad = jnp.zeros((MLA_Q_RANK, MLA_HEADS, LANES - dqk), F32)
    zq_nope = jnp.zeros((MLA_Q_RANK, MLA_HEADS, MLA_NOPE), F32)
    q_plain = jnp.concatenate([wq, zq_pad], axis=2).reshape(MLA_Q_RANK, QK_W)
    q_rot = jnp.concatenate([zq_nope, _rot_half(wq[:, :, MLA_NOPE:]), zq_pad], axis=2).reshape(MLA_Q_RANK, QK_W)
    wuq = jnp.concatenate([q_plain, q_rot], axis=1).astype(BF16)

    wkv = w_ukv.reshape(MLA_KV_RANK, MLA_HEADS, MLA_NOPE + MLA_V)
    zk = jnp.zeros((MLA_KV_RANK, MLA_HEADS, LANES - MLA_NOPE), F32)
    k_pad = jnp.concatenate([wkv[:, :, :MLA_NOPE], zk], axis=2).reshape(MLA_KV_RANK, QK_W)
    v_cols = wkv[:, :, MLA_NOPE:].reshape(MLA_KV_RANK, MLA_HEADS * MLA_V)
    wukv = jnp.concatenate([k_pad, v_cols], axis=1).astype(BF16)
    return w1, wuq, wukv


def _row(v):
    return v.reshape(1, -1)


def kernel(x, mem, norm_mix, w_in, mla_q_norm, mla_kv_norm, mla_w_uq, mla_w_ukv, grp_norm_mla,
           grp_norm_dil, w_out, norm_mem_q, norm_mem_kv, w_mem_q, w_mem_kv, w_mem_o, norm_ffn,
           peer_w_q, peer_sub_keys, peer_u, peer_v, norm_final):
    B, S, D = x.shape
    T = B * S
    tm = min(512, S)
    depth = norm_mix.shape[0]
    tab = _rope_tables(S)
    h = x.reshape(T, D)
    for l in range(depth):
        w1, wuq, wukv = _prep_weights(w_in[l], mla_w_uq[l], mla_w_ukv[l])
        dqkv, q, k, v = _inproj(h, _row(norm_mix[l]), w1, _row(mla_q_norm[l]), _row(mla_kv_norm[l]),
                                wuq, wukv, tab, S, tm)
        o_mla = _mla_attn(q.reshape(B, S, -1), k.reshape(B, S, -1), v.reshape(B, S, -1), tm)
        o_dil = _dil_attn(dqkv.reshape(B, S, -1))
        km, vm = _memkv(mem, _row(norm_mem_kv[l]), w_mem_kv[l].astype(BF16))
        h2 = _mixmem(h, o_mla.reshape(T, -1), o_dil.reshape(T, -1), _row(grp_norm_mla[l]),
                     _row(grp_norm_dil[l]), w_out[l].astype(BF16), _row(norm_mem_q[l]),
                     w_mem_q[l].astype(BF16), km, vm, w_mem_o[l].astype(BF16), S, tm)
        h = _peer(h2, _row(norm_ffn[l]), peer_w_q[l].T.astype(BF16), peer_sub_keys[l].astype(BF16),
                  peer_u[l].astype(BF16), peer_v[l].T.astype(BF16), _row(norm_final), tm, 512,
                  final=l == depth - 1)
    return h.reshape(B, S, D)
```

```python
import functools

import jax
import jax.numpy as jnp
import numpy as np
from jax import lax
from jax.experimental import pallas as pl
from jax.experimental.pallas import tpu as pltpu

EPS = 1e-6
LANES = 128
NEG = -1e30

MLA_HEADS = 8
MLA_NOPE = 64
MLA_ROPE = 32
MLA_V = 64
MLA_Q_RANK = 384
MLA_KV_RANK = 256
ROPE_THETA = 10000.0
DIL_HEADS = 8
DIL_HEAD_DIM = 64
DIL_CONFIGS = ((128, 1), (512, 4), (2048, 16))
DIL_WIDTH = DIL_HEADS * DIL_HEAD_DIM
MEM_HEADS = 4
PEER_HEADS = 8
PEER_KEYS = 128
PEER_QDIM = 256
PEER_TOPK = 16

VMEM_LIMIT = 56 * 1024 * 1024

F32 = jnp.float32
BF16 = jnp.bfloat16


def _rms(xf, g):
    ms = jnp.mean(xf * xf, axis=-1, keepdims=True)
    return xf * lax.rsqrt(ms + EPS) * g


def _dot(a, b):
    return jnp.dot(a, b, preferred_element_type=F32)


def _dot_nt(a, b):
    return lax.dot_general(a, b, (((1,), (1,)), ((), ())), preferred_element_type=F32)


def _params(*sem):
    return pltpu.CompilerParams(dimension_semantics=sem, vmem_limit_bytes=VMEM_LIMIT)


N_DQKV = 3 * DIL_WIDTH
O_CQ = N_DQKV
O_CKV = O_CQ + MLA_Q_RANK
O_KR1 = O_CKV + MLA_KV_RANK
O_KR2 = O_KR1 + LANES
N_IN = O_KR2 + LANES
QK_W = MLA_HEADS * LANES


def _inproj_kernel(x_ref, g_ref, w1_ref, qg_ref, kvg_ref, wuq_ref, wukv_ref, tab_ref,
                   dqkv_ref, q_ref, k_ref, v_ref):
    xn = _rms(x_ref[...], g_ref[...]).astype(BF16)
    proj = _dot(xn, w1_ref[...])
    dqkv_ref[...] = proj[:, :N_DQKV]
    cqn = _rms(proj[:, O_CQ:O_CKV], qg_ref[...]).astype(BF16)
    ckvn = _rms(proj[:, O_CKV:O_KR1], kvg_ref[...]).astype(BF16)
    tab = tab_ref[...]
    q2 = _dot(cqn, wuq_ref[...])
    cos_q = jnp.tile(tab[:, 0:LANES], (1, MLA_HEADS))
    sin_q = jnp.tile(tab[:, LANES:2 * LANES], (1, MLA_HEADS))
    q_ref[...] = (q2[:, :QK_W] * cos_q + q2[:, QK_W:] * sin_q).astype(BF16)
    kv = _dot(ckvn, wukv_ref[...])
    kr = (proj[:, O_KR1:O_KR2] * tab[:, 2 * LANES:3 * LANES]
          + proj[:, O_KR2:N_IN] * tab[:, 3 * LANES:4 * LANES])
    k_ref[...] = (kv[:, :QK_W] + jnp.tile(kr, (1, MLA_HEADS))).astype(BF16)
    v_ref[...] = kv[:, QK_W:].astype(BF16)


def _inproj(x2, g_mix, w1, qg, kvg, wuq, wukv, tab, seq, tm):
    T, D = x2.shape
    nblk_seq = seq // tm
    full = lambda a: pl.BlockSpec(a.shape, lambda i: (0, 0))
    return pl.pallas_call(
        _inproj_kernel,
        grid=(T // tm,),
        in_specs=[pl.BlockSpec((tm, D), lambda i: (i, 0)), full(g_mix), full(w1), full(qg),
                  full(kvg), full(wuq), full(wukv),
                  pl.BlockSpec((tm, 4 * LANES), lambda i: (i % nblk_seq, 0))],
        out_specs=[pl.BlockSpec((tm, N_DQKV), lambda i: (i, 0)),
                   pl.BlockSpec((tm, QK_W), lambda i: (i, 0)),
                   pl.BlockSpec((tm, QK_W), lambda i: (i, 0)),
                   pl.BlockSpec((tm, MLA_HEADS * MLA_V), lambda i: (i, 0))],
        out_shape=[jax.ShapeDtypeStruct((T, N_DQKV), F32),
                   jax.ShapeDtypeStruct((T, QK_W), BF16),
                   jax.ShapeDtypeStruct((T, QK_W), BF16),
                   jax.ShapeDtypeStruct((T, MLA_HEADS * MLA_V), BF16)],
        compiler_params=_params("arbitrary"),
        name="inproj",
    )(x2, g_mix, w1, qg, kvg, wuq, wukv, tab)


def _mla_attn_kernel(q_ref, k_ref, v_ref, o_ref, *, tq):
    i = pl.program_id(2)
    qs = [q_ref[:, h * LANES:(h + 1) * LANES] for h in range(2)]

    def step(j, carry, diagonal):
        start = pl.multiple_of(j * tq, tq)
        vb = v_ref[pl.ds(start, tq), :]
        out = []
        for h in range(2):
            m, l, acc = carry[h]
            kb = k_ref[pl.ds(start, tq), h * LANES:(h + 1) * LANES]
            s = _dot_nt(qs[h], kb)
            if diagonal:
                row = lax.broadcasted_iota(jnp.int32, s.shape, 0)
                col = lax.broadcasted_iota(jnp.int32, s.shape, 1)
                s = jnp.where(col <= row, s, NEG)
            m_new = jnp.maximum(m, jnp.max(s, axis=-1, keepdims=True))
            a = jnp.exp2(m - m_new)
            p = jnp.exp2(s - m_new)
            l = a * l + jnp.sum(p, axis=-1, keepdims=True)
            acc = a * acc + _dot(p.astype(BF16), vb)
            out.append((m_new, l, acc))
        return tuple(out)

    init1 = (jnp.full((tq, 1), NEG, F32), jnp.zeros((tq, 1), F32), jnp.zeros((tq, LANES), F32))
    carry = lax.fori_loop(0, i, functools.partial(step, diagonal=False), (init1, init1))
    (_, l0, acc0), (_, l1, acc1) = step(i, carry, True)
    lane = lax.broadcasted_iota(jnp.int32, (tq, LANES), 1)
    o_ref[...] = jnp.where(lane < MLA_V, acc0 / l0, acc1 / l1)


def _mla_attn(q3, k3, v3, tq):
    B, S, _ = q3.shape
    return pl.pallas_call(
        functools.partial(_mla_attn_kernel, tq=tq),
        grid=(B, MLA_HEADS // 2, S // tq),
        in_specs=[pl.BlockSpec((None, tq, 2 * LANES), lambda b, hp, i: (b, i, hp)),
                  pl.BlockSpec((None, S, 2 * LANES), lambda b, hp, i: (b, 0, hp)),
                  pl.BlockSpec((None, S, LANES), lambda b, hp, i: (b, 0, hp))],
        out_specs=pl.BlockSpec((None, tq, LANES), lambda b, hp, i: (b, i, hp)),
        out_shape=jax.ShapeDtypeStruct((B, S, MLA_HEADS * MLA_V), F32),
        compiler_params=_params("arbitrary", "arbitrary", "arbitrary"),
        name="mla_attn",
    )(q3, k3, v3)


DIL_W = 128
DIL_UNROLL = 4


def _dil_attn_kernel(q_ref, k_ref, v_ref, o_ref, ob_s, lse_s, *, seq):
    hp = pl.program_id(1)
    W = DIL_W
    lane_q = lax.broadcasted_iota(jnp.int32, (W, LANES), 1)
    qi = lax.broadcasted_iota(jnp.int32, (2 * W, 2 * W), 0) % W
    kj = lax.broadcasted_iota(jnp.int32, (2 * W, 2 * W), 1)
    second = lax.broadcasted_iota(jnp.int32, (2 * W, 2 * W), 0) >= W
    steps = W + qi - kj
    valid = (steps >= 0) & (steps <= W)
    slope0 = jnp.exp2(-(2.0 * hp.astype(F32) + 1.0))
    slope = jnp.where(second, 0.5 * slope0, slope0)
    scale = DIL_HEAD_DIM ** -0.5

    for g, (window, d) in enumerate(DIL_CONFIGS):
        assert window // d == W
        nb = seq // d // W
        bias = jnp.where(valid, -slope * (steps * d).astype(F32), NEG)

        def rows(start, d=d):
            return pl.ds(start, W) if d == 1 else pl.ds(start, W, stride=d)

        def block(mb, d=d, nb=nb, bias=bias):
            r = mb // nb
            n = mb % nb
            cur = r + d * W * n
            prev = jnp.where(n > 0, cur - d * W, cur)
            q = q_ref[rows(cur), :] * scale
            qs = jnp.concatenate([jnp.where(lane_q < DIL_HEAD_DIM, q, 0.0),
                                  jnp.where(lane_q < DIL_HEAD_DIM, 0.0, q)], axis=0).astype(BF16)
            kb = jnp.concatenate([k_ref[rows(prev), :], k_ref[rows(cur), :]], axis=0).astype(BF16)
            vb = jnp.concatenate([v_ref[rows(prev), :], v_ref[rows(cur), :]], axis=0).astype(BF16)
            s = _dot_nt(qs, kb) + bias
            s = jnp.where((kj >= W) | (n > 0), s, NEG)
            m = jnp.max(s, axis=-1, keepdims=True)
            p = jnp.exp(s - m)
            l = jnp.sum(p, axis=-1, keepdims=True)
            o = _dot(p.astype(BF16), vb) / l
            lse = m + jnp.log(l)
            o2 = jnp.where(lane_q < DIL_HEAD_DIM, o[:W], o[W:])
            lse2 = jnp.where(lane_q < DIL_HEAD_DIM, lse[:W], lse[W:])
            return cur, o2, lse2

        def blocks(it, carry, g=g):
            done = [block(it * DIL_UNROLL + u) for u in range(DIL_UNROLL)]
            for cur, o2, lse2 in done:
                ob_s[g, rows(cur), :] = o2
                lse_s[g, rows(cur), :] = lse2
            return carry

        assert (d * nb) % DIL_UNROLL == 0
        lax.fori_loop(0, d * nb // DIL_UNROLL, blocks, 0)

    l0, l1, l2 = lse_s[0], lse_s[1], lse_s[2]
    m = jnp.maximum(jnp.maximum(l0, l1), l2)
    w0, w1, w2 = jnp.exp(l0 - m), jnp.exp(l1 - m), jnp.exp(l2 - m)
    o_ref[...] = (w0 * ob_s[0] + w1 * ob_s[1] + w2 * ob_s[2]) / (w0 + w1 + w2)


def _dil_attn(dqkv3):
    B, S, _ = dqkv3.shape
    npair = DIL_HEADS // 2
    return pl.pallas_call(
        functools.partial(_dil_attn_kernel, seq=S),
        grid=(B, npair),
        in_specs=[pl.BlockSpec((None, S, LANES), lambda b, hp: (b, 0, hp)),
                  pl.BlockSpec((None, S, LANES), lambda b, hp: (b, 0, npair + hp)),
                  pl.BlockSpec((None, S, LANES), lambda b, hp: (b, 0, 2 * npair + hp))],
        out_specs=pl.BlockSpec((None, S, LANES), lambda b, hp: (b, 0, hp)),
        out_shape=jax.ShapeDtypeStruct((B, S, DIL_WIDTH), F32),
        scratch_shapes=[pltpu.VMEM((len(DIL_CONFIGS), S, LANES), F32),
                        pltpu.VMEM((len(DIL_CONFIGS), S, LANES), F32)],
        compiler_params=_params("arbitrary", "arbitrary"),
        name="dil_attn",
    )(dqkv3, dqkv3, dqkv3)


def _memkv_kernel(mem_ref, g_ref, w_ref, k_ref, v_ref):
    memn = _rms(mem_ref[...], g_ref[...]).astype(BF16)
    kv = _dot(memn, w_ref[...])
    D = k_ref.shape[-1]
    k_ref[...] = kv[:, :D].astype(BF16)
    v_ref[...] = kv[:, D:].astype(BF16)


def _memkv(mem, g, w):
    B, M, D = mem.shape
    return pl.pallas_call(
        _memkv_kernel,
        grid=(B,),
        in_specs=[pl.BlockSpec((None, M, D), lambda b: (b, 0, 0)),
                  pl.BlockSpec(g.shape, lambda b: (0, 0)),
                  pl.BlockSpec(w.shape, lambda b: (0, 0))],
        out_specs=[pl.BlockSpec((None, M, D), lambda b: (b, 0, 0)),
                   pl.BlockSpec((None, M, D), lambda b: (b, 0, 0))],
        out_shape=[jax.ShapeDtypeStruct((B, M, D), BF16), jax.ShapeDtypeStruct((B, M, D), BF16)],
        compiler_params=_params("arbitrary"),
        name="memkv",
    )(mem, g, w)


def _mixmem_kernel(x_ref, om_ref, od_ref, gm_ref, gd_ref, wo_ref, gq_ref, wq_ref, km_ref, vm_ref,
                   wmo_ref, h_ref):
    half = om_ref.shape[-1]
    mm = _rms(om_ref[...], gm_ref[...]).astype(BF16)
    md = _rms(od_ref[...], gd_ref[...]).astype(BF16)
    h1 = x_ref[...] + _dot(mm, wo_ref[:half, :]) + _dot(md, wo_ref[half:, :])
    hn = _rms(h1, gq_ref[...]).astype(BF16)
    D = h1.shape[-1]
    dh = D // MEM_HEADS
    q = (_dot(hn, wq_ref[...]) * dh ** -0.5).astype(BF16)
    outs = []
    for h in range(MEM_HEADS):
        s = _dot_nt(q[:, h * dh:(h + 1) * dh], km_ref[:, h * dh:(h + 1) * dh])
        m = jnp.max(s, axis=-1, keepdims=True)
        p = jnp.exp(s - m)
        l = jnp.sum(p, axis=-1, keepdims=True)
        outs.append((_dot(p.astype(BF16), vm_ref[:, h * dh:(h + 1) * dh]) / l).astype(BF16))
    o = jnp.concatenate(outs, axis=-1)
    h_ref[...] = h1 + _dot(o, wmo_ref[...])


def _mixmem(x2, om, od, gm, gd, wo, gq, wq, km, vm, wmo, seq, tm):
    T, D = x2.shape
    M = km.shape[1]
    nblk_seq = seq // tm
    full = lambda a: pl.BlockSpec(a.shape, lambda i: (0, 0))
    return pl.pallas_call(
        _mixmem_kernel,
        grid=(T // tm,),
        in_specs=[pl.BlockSpec((tm, D), lambda i: (i, 0)),
                  pl.BlockSpec((tm, om.shape[1]), lambda i: (i, 0)),
                  pl.BlockSpec((tm, od.shape[1]), lambda i: (i, 0)),
                  full(gm), full(gd), full(wo), full(gq), full(wq),
                  pl.BlockSpec((None, M, D), lambda i: (i // nblk_seq, 0, 0)),
                  pl.BlockSpec((None, M, D), lambda i: (i // nblk_seq, 0, 0)),
                  full(wmo)],
        out_specs=pl.BlockSpec((tm, D), lambda i: (i, 0)),
        out_shape=jax.ShapeDtypeStruct((T, D), F32),
        compiler_params=_params("arbitrary"),
        name="mixmem",
    )(x2, om, od, gm, gd, wo, gq, wq, km, vm, wmo)


PEER_BLOCK = 512
PEER_BLOCKS_PER_STEP = 4
SUBLANES = 8
BF16_ROWS = 16


def _batcher_pairs(n):
    pairs, p = [], 1
    while p < n:
        k = p
        while k >= 1:
            for j in range(k % p, n - k, 2 * k):
                for i in range(min(k, n - j - k)):
                    if (i + j) // (2 * p) == (i + j + k) // (2 * p):
                        pairs.append((i + j, i + j + k))
            k //= 2
        p *= 2
    return pairs


def _bitonic_pairs(n):
    pairs, s = [], n // 2
    while s >= 1:
        pairs += [(i, i + s) for i in range(n) if i & s == 0]
        s //= 2
    return pairs


def _apply_desc(v, pairs):
    for i, j in pairs:
        v[i], v[j] = jnp.maximum(v[i], v[j]), jnp.minimum(v[i], v[j])
    return v


def _merge_sublanes(v, shifts):
    n = len(v)
    for shift in shifts:
        w = [pltpu.roll(x, shift, axis=0) for x in v]
        v = [jnp.maximum(v[k], w[n - 1 - k]) for k in range(n)]
        v = _apply_desc(v, _bitonic_pairs(n))
    return v


def _top16_of_keys(s):
    v = [s[SUBLANES * k:SUBLANES * (k + 1)] for k in range(s.shape[0] // SUBLANES)]
    assert len(v) == PEER_TOPK
    v = _apply_desc(v, _batcher_pairs(PEER_TOPK))
    return _merge_sublanes(v, (4, 2, 1))


_CAND_PAIRS = [(k, l) for k in range(PEER_TOPK) for l in range(PEER_TOPK)
               if (k + 1) * (l + 1) <= PEER_TOPK]


def _top16_of_pair_sums(a, b):
    cands = [a[k] + b[l] for (k, l) in _CAND_PAIRS]
    sub = lax.broadcasted_iota(jnp.int32, a[0].shape, 0)
    packed = []
    for m in range(SUBLANES):
        x = jnp.full(a[0].shape, -jnp.inf, F32)
        for s_ in range(SUBLANES):
            idx = SUBLANES * m + s_
            if idx < len(cands):
                x = jnp.where(sub == s_, cands[idx], x)
        packed.append(x)
    assert len(cands) <= SUBLANES * SUBLANES
    packed = _apply_desc(packed, _batcher_pairs(SUBLANES))
    w = [pltpu.roll(x, 4, axis=0) for x in packed]
    v = _apply_desc(packed + w[::-1], _bitonic_pairs(2 * SUBLANES))
    return _merge_sublanes(v, (2, 1))


def _peer_kernel(h_ref, gf_ref, wqt_ref, keys_ref, u_ref, vt_ref, gfin_ref, out_ref,
                 xnt_s, rank_s, e1_s, cnt_s, e0_s, acc_s, *, eb, nblk, final):
    e = pl.program_id(1)
    n_e = pl.num_programs(1)
    half = PEER_QDIM // 2
    nk = PEER_KEYS
    tt = xnt_s.shape[-1]
    sub_blocks = eb // nk
    zero = jnp.zeros((), BF16)

    def row16(ref, hh, i):
        return jnp.broadcast_to(ref[hh, pl.ds(i, 1), :], (BF16_ROWS, tt)).astype(BF16)[None]

    def gate_block(blk):
        rows = []
        for ii in range(sub_blocks):
            i = blk * sub_blocks + ii
            c = None
            for hh in range(PEER_HEADS):
                term = row16(e0_s, hh, i) * jnp.where(rank_s[hh] < row16(cnt_s, hh, i), e1_s[hh], zero)
                c = term if c is None else c + term
            rows.append(c.reshape(nk, tt))
        return jnp.concatenate(rows, axis=0)

    @pl.when(e == 0)
    def _prologue():
        xn = _rms(h_ref[...], gf_ref[...])
        xnt = xn.T.astype(BF16)
        xnt_s[...] = xnt
        acc_s[...] = jnp.zeros_like(acc_s)
        for hh in range(PEER_HEADS):
            qt = _dot(wqt_ref[hh * PEER_QDIM:(hh + 1) * PEER_QDIM, :], xnt)
            s0 = _dot(keys_ref[hh, 0], qt[:half].astype(BF16))
            s1 = _dot(keys_ref[hh, 1], qt[half:].astype(BF16))
            a = _top16_of_keys(s0)
            b = _top16_of_keys(s1)
            top = _top16_of_pair_sums(a, b)
            z = sum(jnp.exp(t - top[0]) for t in top)
            tau = top[PEER_TOPK - 1]
            s0 = s0.reshape(nk // SUBLANES, SUBLANES, tt)
            s1 = s1.reshape(nk // SUBLANES, SUBLANES, tt)
            cnt = jnp.zeros_like(s0)
            rank = jnp.zeros_like(s1)
            for r in range(PEER_TOPK):
                cnt = jnp.where(s0 + b[r][None] >= tau[None], r + 1.0, cnt)
                rank = jnp.where(b[r][None] > s1, r + 1.0, rank)
            cnt_s[hh] = cnt.reshape(nk, tt)
            e0_s[hh] = (jnp.exp(s0 - a[0][None]) / z[None]).reshape(nk, tt)
            rank_s[hh] = rank.reshape(nk // BF16_ROWS, BF16_ROWS, tt).astype(BF16)
            e1_s[hh] = jnp.exp(s1 - b[0][None]).reshape(nk // BF16_ROWS, BF16_ROWS, tt).astype(BF16)


    at_next = _dot(u_ref[0:eb, :], xnt_s[...])
    for blk in range(nblk):
        at = at_next
        if blk + 1 < nblk:
            at_next = _dot(u_ref[(blk + 1) * eb:(blk + 2) * eb, :], xnt_s[...])
        gl = 0.5 * at * (1.0 + lax.erf(at * np.float32(np.sqrt(0.5))))
        ct = gate_block(e * nblk + blk) * gl.astype(BF16)
        acc_s[...] += _dot(vt_ref[:, blk * eb:(blk + 1) * eb], ct)

    @pl.when(e == n_e - 1)
    def _epilogue():
        y = h_ref[...] + acc_s[...].T
        out_ref[...] = _rms(y, gfin_ref[...]) if final else y


def _peer(h2, gf, wqt, keys, u, vt, gfin, tt, eb, nblk, final):
    T, D = h2.shape
    n_exp = u.shape[0]
    return pl.pallas_call(
        functools.partial(_peer_kernel, eb=eb, nblk=nblk, final=final),
        grid=(T // tt, n_exp // (nblk * eb)),
        in_specs=[pl.BlockSpec((tt, D), lambda t, e: (t, 0)),
                  pl.BlockSpec(gf.shape, lambda t, e: (0, 0)),
                  pl.BlockSpec(wqt.shape, lambda t, e: (0, 0)),
                  pl.BlockSpec(keys.shape, lambda t, e: (0, 0, 0, 0)),
                  pl.BlockSpec((nblk * eb, D), lambda t, e: (e, 0)),
                  pl.BlockSpec((D, nblk * eb), lambda t, e: (0, e)),
                  pl.BlockSpec(gfin.shape, lambda t, e: (0, 0))],
        out_specs=pl.BlockSpec((tt, D), lambda t, e: (t, 0)),
        out_shape=jax.ShapeDtypeStruct((T, D), F32),
        scratch_shapes=[pltpu.VMEM((D, tt), BF16),
                        pltpu.VMEM((PEER_HEADS, PEER_KEYS // BF16_ROWS, BF16_ROWS, tt), BF16),
                        pltpu.VMEM((PEER_HEADS, PEER_KEYS // BF16_ROWS, BF16_ROWS, tt), BF16),
                        pltpu.VMEM((PEER_HEADS, PEER_KEYS, tt), F32),
                        pltpu.VMEM((PEER_HEADS, PEER_KEYS, tt), F32),
                        pltpu.VMEM((D, tt), F32)],
        compiler_params=_params("arbitrary", "arbitrary"),
        name="peer",
    )(h2, gf, wqt, keys, u, vt, gfin)


def _rope_tables(seq):
    half = MLA_ROPE // 2
    inv = 1.0 / (ROPE_THETA ** (jnp.arange(0, MLA_ROPE, 2, dtype=F32) / MLA_ROPE))
    ang = jnp.arange(seq, dtype=F32)[:, None] * inv[None, :]
    cos, sin = jnp.cos(ang), jnp.sin(ang)
    scale = float((MLA_NOPE + MLA_ROPE) ** -0.5 * np.log2(np.e))
    ones = jnp.ones((seq, MLA_NOPE), F32)
    z_nope = jnp.zeros((seq, MLA_NOPE), F32)
    z_pad = jnp.zeros((seq, LANES - MLA_NOPE - MLA_ROPE), F32)
    cos_q = jnp.concatenate([ones, cos, cos, z_pad], axis=1) * scale
    sin_q = jnp.concatenate([z_nope, sin, sin, z_pad], axis=1) * scale
    cos_k = jnp.concatenate([z_nope, cos, cos, z_pad], axis=1)
    sin_k = jnp.concatenate([z_nope, sin, sin, z_pad], axis=1)
    del half
    return jnp.concatenate([cos_q, sin_q, cos_k, sin_k], axis=1)


def _rot_half(w):
    h = w.shape[-1] // 2
    return jnp.concatenate([-w[..., h:], w[..., :h]], axis=-1)


def _prep_weights(w_in, w_uq, w_ukv):
    D = w_in.shape[0]
    o1 = MLA_Q_RANK
    o2 = o1 + MLA_KV_RANK
    o3 = o2 + MLA_ROPE
    w_cq, w_ckv, w_kr, w_dqkv = w_in[:, :o1], w_in[:, o1:o2], w_in[:, o2:o3], w_in[:, o3:]
    lead = jnp.zeros((D, MLA_NOPE), F32)
    tail = jnp.zeros((D, LANES - MLA_NOPE - MLA_ROPE), F32)
    kr1 = jnp.concatenate([lead, w_kr, tail], axis=1)
    kr2 = jnp.concatenate([lead, _rot_half(w_kr), tail], axis=1)
    w1 = jnp.concatenate([w_dqkv, w_cq, w_ckv, kr1, kr2], axis=1).astype(BF16)

    dqk = MLA_NOPE + MLA_ROPE
    wq = w_uq.reshape(MLA_Q_RANK, MLA_HEADS, dqk)
    zq_pad = jnp.zeros((MLA_Q_RANK, MLA_HEADS, LANES - dqk), F32)
    zq_nope = jnp.zeros((MLA_Q_RANK, MLA_HEADS, MLA_NOPE), F32)
    q_plain = jnp.concatenate([wq, zq_pad], axis=2).reshape(MLA_Q_RANK, QK_W)
    q_rot = jnp.concatenate([zq_nope, _rot_half(wq[:, :, MLA_NOPE:]), zq_pad], axis=2).reshape(MLA_Q_RANK, QK_W)
    wuq = jnp.concatenate([q_plain, q_rot], axis=1).astype(BF16)

    wkv = w_ukv.reshape(MLA_KV_RANK, MLA_HEADS, MLA_NOPE + MLA_V)
    zk = jnp.zeros((MLA_KV_RANK, MLA_HEADS, LANES - MLA_NOPE), F32)
    k_pad = jnp.concatenate([wkv[:, :, :MLA_NOPE], zk], axis=2).reshape(MLA_KV_RANK, QK_W)
    v_cols = wkv[:, :, MLA_NOPE:].reshape(MLA_KV_RANK, MLA_HEADS * MLA_V)
    wukv = jnp.concatenate([k_pad, v_cols], axis=1).astype(BF16)
    return w1, wuq, wukv


def _row(v):
    return v.reshape(1, -1)


def kernel(x, mem, norm_mix, w_in, mla_q_norm, mla_kv_norm, mla_w_uq, mla_w_ukv, grp_norm_mla,
           grp_norm_dil, w_out, norm_mem_q, norm_mem_kv, w_mem_q, w_mem_kv, w_mem_o, norm_ffn,
           peer_w_q, peer_sub_keys, peer_u, peer_v, norm_final):
    B, S, D = x.shape
    T = B * S
    tm = min(512, S)
    depth = norm_mix.shape[0]
    tab = _rope_tables(S)
    h = x.reshape(T, D)
    for l in range(depth):
        w1, wuq, wukv = _prep_weights(w_in[l], mla_w_uq[l], mla_w_ukv[l])
        dqkv, q, k, v = _inproj(h, _row(norm_mix[l]), w1, _row(mla_q_norm[l]), _row(mla_kv_norm[l]),
                                wuq, wukv, tab, S, tm)
        o_mla = _mla_attn(q.reshape(B, S, -1), k.reshape(B, S, -1), v.reshape(B, S, -1), tm)
        o_dil = _dil_attn(dqkv.reshape(B, S, -1))
        km, vm = _memkv(mem, _row(norm_mem_kv[l]), w_mem_kv[l].astype(BF16))
        h2 = _mixmem(h, o_mla.reshape(T, -1), o_dil.reshape(T, -1), _row(grp_norm_mla[l]),
                     _row(grp_norm_dil[l]), w_out[l].astype(BF16), _row(norm_mem_q[l]),
                     w_mem_q[l].astype(BF16), km, vm, w_mem_o[l].astype(BF16), S, tm)
        h = _peer(h2, _row(norm_ffn[l]), peer_w_q[l].T.astype(BF16), peer_sub_keys[l].astype(BF16),
                  peer_u[l].astype(BF16), peer_v[l].T.astype(BF16), _row(norm_final), tm,
                  PEER_BLOCK, PEER_BLOCKS_PER_STEP, final=l == depth - 1)
    return h.reshape(B, S, D)
```
